```python
import jax, jax.numpy as jnp
from jax import lax
import numpy as np

D_MODEL = 1024
BATCH = 8
SEQ = 2048
DEPTH = 1

CTX_LEN = 256
GRID_W = 64
D_CONV = D_MODEL
CONV_K = 3
D_LSTM = D_MODEL
N_HEADS = 8
HEAD_DIM = D_LSTM // N_HEADS
CHUNK = 64
EPS = 1e-6
F_BIAS_LO = 3.0
F_BIAS_HI = 6.0
STATE_COLS = 3 * D_LSTM + 4 * N_HEADS
N_IN = STATE_COLS + 2 * D_LSTM + 4 * D_CONV + 2 * D_MODEL

kernel_name = "hybrid_conv_mlstm_prefix_block"


def _split_sizes(a, sizes):
    idx, acc = [], 0
    for s in sizes[:-1]:
        acc += s
        idx.append(acc)
    return jnp.split(a, idx, axis=-1)


def rmsnorm(a, g):
    af = a.astype(jnp.float32)
    y = af * lax.rsqrt(jnp.mean(af * af, axis=-1, keepdims=True) + EPS) * g.astype(jnp.float32)
    return y.astype(a.dtype)


def dwconv3(u, w, b):
    up = jnp.pad(u, [(0, 0)] * (u.ndim - 2) + [(1, 1), (0, 0)])
    return w[0] * up[..., :-2, :] + w[1] * up[..., 1:-1, :] + w[2] * up[..., 2:, :] + b


def mlstm_scan(q, k, v, i_pre, logf, state, emit):
    bsz, nh, length, dh = q.shape
    nc = length // CHUNK

    def chunks(a):
        return jnp.moveaxis(a.reshape(bsz, nh, nc, CHUNK, *a.shape[3:]), 2, 0)

    tri = jnp.tril(jnp.ones((CHUNK, CHUNK), dtype=bool))

    def step(carry, inp):
        C, n, m = carry
        qc, kc, vc, ic, fc = inp
        b = jnp.cumsum(fc, axis=-1)
        b_last = b[..., -1]
        w_log = b_last[..., None] - b + ic
        m_new = jnp.maximum(b_last + m, w_log.max(-1))
        wk = jnp.exp(w_log - m_new[..., None])
        decay = jnp.exp(b_last + m - m_new)
        C_new = decay[..., None, None] * C + jnp.einsum('bhj,bhjd,bhje->bhde', wk, kc, vc)
        n_new = decay[..., None] * n + jnp.einsum('bhj,bhjd->bhd', wk, kc)
        if not emit:
            return (C_new, n_new, m_new), None
        log_w = jnp.where(tri, b[..., :, None] - b[..., None, :] + ic[..., None, :], -jnp.inf)
        log_inter = b + m[..., None]
        m_row = jnp.maximum(log_w.max(-1), log_inter)
        s = jnp.einsum('bhid,bhjd->bhij', qc, kc) * jnp.exp(log_w - m_row[..., None])
        w_inter = jnp.exp(log_inter - m_row)
        num = jnp.einsum('bhij,bhje->bhie', s, vc) + w_inter[..., None] * jnp.einsum('bhid,bhde->bhie', qc, C)
        den = s.sum(-1) + w_inter * jnp.einsum('bhid,bhd->bhi', qc, n)
        den = jnp.maximum(jnp.abs(den), jnp.exp(-m_row))
        return (C_new, n_new, m_new), num / den[..., None]

    state, hs = lax.scan(step, state, tuple(chunks(a) for a in (q, k, v, i_pre, logf)))
    if not emit:
        return None, state
    return jnp.moveaxis(hs, 0, 2).reshape(bsz, nh, length, dh), state


def zero_state(bsz):
    return (jnp.zeros((bsz, N_HEADS, HEAD_DIM, HEAD_DIM), jnp.float32),
            jnp.zeros((bsz, N_HEADS, HEAD_DIM), jnp.float32),
            jnp.zeros((bsz, N_HEADS), jnp.float32))


def mlstm_bidir(p_state, states, emit):
    q, k, v, gates = _split_sizes(p_state, [D_LSTM, D_LSTM, D_LSTM, 4 * N_HEADS])

    def heads(a):
        return a.reshape(a.shape[0], a.shape[1], N_HEADS, HEAD_DIM).transpose(0, 2, 1, 3).astype(jnp.float32)

    qh = heads(q) * HEAD_DIM ** -0.5
    kh, vh = heads(k), heads(v)
    g = gates.astype(jnp.float32).transpose(0, 2, 1)
    i_f, f_f, i_b, f_b = jnp.split(g, 4, axis=1)
    h_f, st_f = mlstm_scan(qh, kh, vh, i_f, jax.nn.log_sigmoid(f_f), states[0], emit)
    rev = lambda a: jnp.flip(a, axis=2)
    h_b, st_b = mlstm_scan(rev(qh), rev(kh), rev(vh), rev(i_b), rev(jax.nn.log_sigmoid(f_b)), states[1], emit)
    if not emit:
        return None, (st_f, st_b)
    return h_f + rev(h_b), (st_f, st_b)


def mixer_output(h_lstm, p_rest, grid, w_conv, b_conv, g_head, w_pa, w_pb, w_out, b_out):
    o, z_b, b_a, c_a, x_a, z_a, g_a, g_b = _split_sizes(
        p_rest, [D_LSTM, D_LSTM, D_CONV, D_CONV, D_CONV, D_CONV, D_MODEL, D_MODEL])
    u = c_a * x_a
    if grid:
        bsz, length, ch = u.shape
        conv = dwconv3(u.reshape(bsz, length // GRID_W, GRID_W, ch), w_conv, b_conv).reshape(bsz, length, ch)
    else:
        conv = dwconv3(u, w_conv, b_conv)
    y_a = b_a * conv * jax.nn.silu(z_a)
    hn = h_lstm * lax.rsqrt(jnp.mean(h_lstm * h_lstm, axis=-1, keepdims=True) + EPS)
    bsz, _, length, _ = hn.shape
    hn = (hn.transpose(0, 2, 1, 3).reshape(bsz, length, D_LSTM) * g_head.astype(jnp.float32)).astype(p_rest.dtype)
    y_b = jax.nn.sigmoid(o) * hn * jax.nn.silu(z_b)
    merged = jax.nn.sigmoid(g_a) * (y_a @ w_pa) + jax.nn.sigmoid(g_b) * (y_b @ w_pb)
    return merged @ w_out + b_out


def setup_inputs(seed: int = 0) -> dict:
    key = jax.random.key(seed)
    ks = jax.random.split(key, 17)
    nrm = lambda k, shape, s: jax.random.normal(k, shape, jnp.float32) * s
    b_in = nrm(ks[8], (DEPTH, N_IN), 0.02)
    f_bias = jnp.linspace(F_BIAS_LO, F_BIAS_HI, N_HEADS, dtype=jnp.float32)
    og = 3 * D_LSTM
    b_in = b_in.at[:, og + N_HEADS:og + 2 * N_HEADS].add(f_bias)
    b_in = b_in.at[:, og + 3 * N_HEADS:og + 4 * N_HEADS].add(f_bias)
    return {
        "x": nrm(ks[0], (BATCH, SEQ, D_MODEL), 1.0),
        "c": nrm(ks[1], (BATCH, D_MODEL), 1.0),
        "ctx": nrm(ks[2], (BATCH, CTX_LEN, D_MODEL), 1.0),
        "c_ctx": nrm(ks[3], (D_MODEL,), 1.0),
        "w_ada": nrm(ks[4], (DEPTH, D_MODEL, 3 * D_MODEL), 0.5 * D_MODEL ** -0.5),
        "b_ada": nrm(ks[5], (DEPTH, 3 * D_MODEL), 0.02),
        "g_norm": 1.0 + nrm(ks[6], (DEPTH, D_MODEL), 0.02),
        "w_in": nrm(ks[7], (DEPTH, D_MODEL, N_IN), D_MODEL ** -0.5),
        "b_in": b_in,
        "w_conv": nrm(ks[9], (DEPTH, CONV_K, D_CONV), CONV_K ** -0.5),
        "b_conv": nrm(ks[10], (DEPTH, D_CONV), 0.02),
        "g_head": 1.0 + nrm(ks[11], (DEPTH, D_LSTM), 0.02),
        "w_pa": nrm(ks[12], (DEPTH, D_CONV, D_MODEL), D_CONV ** -0.5),
        "w_pb": nrm(ks[13], (DEPTH, D_LSTM, D_MODEL), D_LSTM ** -0.5),
        "w_out": nrm(ks[14], (DEPTH, D_MODEL, D_MODEL), D_MODEL ** -0.5),
        "b_out": nrm(ks[15], (DEPTH, D_MODEL), 0.02),
        "g_final": 1.0 + nrm(ks[16], (D_MODEL,), 0.02),
    }


def reference(x, c, ctx, c_ctx, w_ada, b_ada, g_norm, w_in, b_in, w_conv, b_conv, g_head,
              w_pa, w_pb, w_out, b_out, g_final):
    bsz = x.shape[0]
    for l in range(DEPTH):
        last = l == DEPTH - 1
        shift, scale, gate = jnp.split(jax.nn.silu(c) @ w_ada[l] + b_ada[l], 3, axis=-1)
        shift_c, scale_c, gate_c = jnp.split(jax.nn.silu(c_ctx) @ w_ada[l] + b_ada[l], 3, axis=-1)
        hx = rmsnorm(x, g_norm[l]) * (1 + scale[:, None, :]) + shift[:, None, :]
        hc = rmsnorm(ctx, g_norm[l]) * (1 + scale_c) + shift_c
        px = hx @ w_in[l] + b_in[l]
        n_ctx_cols = STATE_COLS if last else N_IN
        pc = hc @ w_in[l, :, :n_ctx_cols] + b_in[l, :n_ctx_cols]
        h_ctx, ctx_states = mlstm_bidir(pc[..., :STATE_COLS], (zero_state(bsz), zero_state(bsz)), not last)
        h_x, _ = mlstm_bidir(px[..., :STATE_COLS], ctx_states, True)
        out_x = mixer_output(h_x, px[..., STATE_COLS:], True, w_conv[l], b_conv[l], g_head[l],
                             w_pa[l], w_pb[l], w_out[l], b_out[l])
        if not last:
            out_c = mixer_output(h_ctx, pc[..., STATE_COLS:], False, w_conv[l], b_conv[l], g_head[l],
                                 w_pa[l], w_pb[l], w_out[l], b_out[l])
            ctx = ctx + gate_c * out_c
        x = x + gate[:, None, :] * out_x
    return rmsnorm(x, g_final)
```

```python
import functools

import jax
import jax.numpy as jnp
from jax import lax
from jax.experimental import pallas as pl
from jax.experimental.pallas import tpu as pltpu

D_MODEL = 1024
N_HEADS = 8
HEAD_DIM = D_MODEL // N_HEADS
EPS = 1e-6
LK = 128
GRID_W = 64
N_GATES = 4 * N_HEADS
STATE_COLS = 3 * D_MODEL + N_GATES
MOD_ROWS = 16

F32 = jnp.float32
BF16 = jnp.bfloat16

_VMEM_LIMIT = 56 * 1024 * 1024


def _dot(a, b):
    return jnp.dot(a, b, preferred_element_type=F32)


def _silu(z):
    return z * jax.nn.sigmoid(z)


def _ada_kernel(c_ref, w_ref, b_ref, o_ref):
    a = _silu(c_ref[...]).astype(BF16)
    o_ref[...] = _dot(a, w_ref[...].astype(BF16)) + b_ref[...]


def _ada_call(cc, w_ada, b_ada):
    n_out = w_ada.shape[1]
    bn = D_MODEL
    return pl.pallas_call(
        _ada_kernel,
        grid=(n_out // bn,),
        in_specs=[
            pl.BlockSpec((MOD_ROWS, D_MODEL), lambda j: (0, 0)),
            pl.BlockSpec((D_MODEL, bn), lambda j: (0, j)),
            pl.BlockSpec((1, bn), lambda j: (0, j)),
        ],
        out_specs=pl.BlockSpec((MOD_ROWS, bn), lambda j: (0, j)),
        out_shape=jax.ShapeDtypeStruct((MOD_ROWS, n_out), F32),
        name="ada_mod",
    )(cc, w_ada, b_ada)


def _modulated_norm(x, gn, mod_ref):
    ms = jnp.mean(x * x, axis=-1, keepdims=True)
    xn = x * lax.rsqrt(ms + EPS) * gn
    shift = mod_ref[0, 0:1, :]
    scale = mod_ref[0, 1:2, :]
    return xn * (1.0 + scale) + shift


def _state_proj_kernel(x_ref, mod_ref, gn_ref, wq_ref, wk_ref, wv_ref, wgt_ref,
                       bq_ref, bk_ref, bv_ref, bg_ref, q_ref, k_ref, v_ref, gt_ref, *, emit_q):
    hx = _modulated_norm(x_ref[0], gn_ref[...], mod_ref).astype(BF16)
    if emit_q:
        q = _dot(hx, wq_ref[...]) + bq_ref[...]
        q_ref[0] = (q * HEAD_DIM ** -0.5).astype(BF16)
    else:
        q_ref[0] = jnp.zeros(q_ref.shape[1:], BF16)
    k_ref[0] = (_dot(hx, wk_ref[...]) + bk_ref[...]).astype(BF16)
    v_ref[0] = (_dot(hx, wv_ref[...]) + bv_ref[...]).astype(BF16)
    gt = lax.dot_general(wgt_ref[...], hx, (((1,), (1,)), ((), ())), preferred_element_type=F32)
    gt_ref[0] = gt + bg_ref[...]


def _state_proj_call(x, mod, gn, wq, wk, wv, wgt, bq, bk, bv, bg, *, tm, emit_q, per_batch_mod):
    bsz, t, d = x.shape
    const = lambda b, i: (0, 0)
    w_spec = pl.BlockSpec((d, d), const, pipeline_mode=pl.Buffered(1))
    row_spec = pl.BlockSpec((1, d), const)
    mod_map = (lambda b, i: (b, 0, 0)) if per_batch_mod else (lambda b, i: (0, 0, 0))
    tok_spec = pl.BlockSpec((1, tm, d), lambda b, i: (b, i, 0))
    q_rows = tm if emit_q else 8
    q_spec = tok_spec if emit_q else pl.BlockSpec((1, q_rows, d), lambda b, i: (0, 0, 0))
    q_shape = (bsz, t, d) if emit_q else (1, q_rows, d)
    return pl.pallas_call(
        functools.partial(_state_proj_kernel, emit_q=emit_q),
        grid=(bsz, t // tm),
        in_specs=[
            tok_spec,
            pl.BlockSpec((1, 3, d), mod_map),
            row_spec,
            w_spec, w_spec, w_spec,
            pl.BlockSpec((N_GATES, d), const),
            row_spec, row_spec, row_spec,
            pl.BlockSpec((N_GATES, 1), const),
        ],
        out_specs=[
            q_spec, tok_spec, tok_spec,
            pl.BlockSpec((1, N_GATES, tm), lambda b, i: (b, 0, i)),
        ],
        out_shape=[
            jax.ShapeDtypeStruct(q_shape, BF16),
            jax.ShapeDtypeStruct((bsz, t, d), BF16),
            jax.ShapeDtypeStruct((bsz, t, d), BF16),
            jax.ShapeDtypeStruct((bsz, N_GATES, t), F32),
        ],
        compiler_params=pltpu.CompilerParams(
            dimension_semantics=("arbitrary", "arbitrary"), vmem_limit_bytes=_VMEM_LIMIT),
        name="state_proj_x" if emit_q else "state_proj_ctx",
    )(x, mod, gn, wq, wk, wv, wgt, bq, bk, bv, bg)


def _log_sigmoid(x):
    return jnp.minimum(x, 0.0) - jnp.log1p(jnp.exp(-jnp.abs(x)))


def _lane_scan(x, op, fill, reverse):
    lane = lax.broadcasted_iota(jnp.int32, x.shape, 1)
    s = 1
    while s < LK:
        if reverse:
            shifted = jnp.where(lane < LK - s, pltpu.roll(x, LK - s, axis=1), fill)
        else:
            shifted = jnp.where(lane >= s, pltpu.roll(x, s, axis=1), fill)
        x = op(x, shifted)
        s *= 2
    return x


def _chunk_rows(gt, row0, nc):
    return jnp.concatenate(
        [gt[row0:row0 + N_HEADS, c * LK:(c + 1) * LK] for c in range(nc)], axis=0)


def _direction_prep(i_pre, f_pre, nc, order, m0, reverse):
    lf = _log_sigmoid(f_pre)
    b = _lane_scan(lf, jnp.add, 0.0, reverse)
    c = i_pre - b
    g = _lane_scan(c, jnp.maximum, -jnp.inf, reverse)
    end = 0 if reverse else LK - 1
    m_in = [None] * nc
    m = m0
    for ci in order:
        r = slice(ci * N_HEADS, (ci + 1) * N_HEADS)
        m_in[ci] = m
        m = b[r, end:end + 1] + jnp.maximum(g[r, end:end + 1], m)
    m_in = jnp.concatenate(m_in, axis=0)
    big_m = jnp.maximum(g, m_in)
    m_end = big_m[:, end:end + 1]
    shape = c.shape
    return dict(
        c=c,
        wk=jnp.exp(c - m_end),
        dec=jnp.broadcast_to(jnp.exp(m_in - m_end), shape),
        wi=jnp.exp(m_in - big_m),
        en=jnp.exp(-(b + big_m)),
        M=big_m,
    ), m


def _scan_kernel(qx_ref, kx_ref, vx_ref, kc_ref, vc_ref, gtx_ref, gtc_ref, o_ref,
                 sf_ref, sb_ref, col_ref, cf_ref, cb_ref, wkf_ref, wkb_ref, decf_ref, decb_ref,
                 cwkf_ref, cwkb_ref, cdecf_ref, cdecb_ref, *, nc_x, nc_c):
    H = N_HEADS
    ones_blk = jnp.ones((LK, HEAD_DIM), BF16)

    gtc = gtc_ref[0]
    gtx = gtx_ref[0]
    zero_m = jnp.zeros((H, 1), F32)
    pf_c, m_f = _direction_prep(_chunk_rows(gtc, 0, nc_c), _chunk_rows(gtc, H, nc_c), nc_c,
                                range(nc_c), zero_m, False)
    pb_c, m_b = _direction_prep(_chunk_rows(gtc, 2 * H, nc_c), _chunk_rows(gtc, 3 * H, nc_c), nc_c,
                                range(nc_c - 1, -1, -1), zero_m, True)
    pf_x, _ = _direction_prep(_chunk_rows(gtx, 0, nc_x), _chunk_rows(gtx, H, nc_x), nc_x,
                              range(nc_x), m_f, False)
    pb_x, _ = _direction_prep(_chunk_rows(gtx, 2 * H, nc_x), _chunk_rows(gtx, 3 * H, nc_x), nc_x,
                              range(nc_x - 1, -1, -1), m_b, True)

    for ci in range(nc_c):
        r = slice(ci * H, (ci + 1) * H)
        cwkf_ref[ci] = pf_c["wk"][r]
        cwkb_ref[ci] = pb_c["wk"][r]
        cdecf_ref[ci] = jnp.concatenate([pf_c["dec"][r]] * 2, axis=1)
        cdecb_ref[ci] = jnp.concatenate([pb_c["dec"][r]] * 2, axis=1)
    pad = jnp.zeros((LK - 6 * H, LK), F32)
    for ci in range(nc_x):
        r = slice(ci * H, (ci + 1) * H)
        cf_ref[ci] = pf_x["c"][r]
        cb_ref[ci] = pb_x["c"][r]
        wkf_ref[ci] = pf_x["wk"][r]
        wkb_ref[ci] = pb_x["wk"][r]
        decf_ref[ci] = jnp.concatenate([pf_x["dec"][r]] * 2, axis=1)
        decb_ref[ci] = jnp.concatenate([pb_x["dec"][r]] * 2, axis=1)
        tile = jnp.concatenate([pf_x["M"][r], pf_x["wi"][r], pf_x["en"][r],
                                pb_x["M"][r], pb_x["wi"][r], pb_x["en"][r], pad], axis=0)
        col_ref[ci] = tile.T

    sf_ref[...] = jnp.zeros(sf_ref.shape, F32)
    sb_ref[...] = jnp.zeros(sb_ref.shape, F32)
    o_ref[...] = jnp.zeros(o_ref.shape, F32)

    def state_update(s_ref, h, kt, v1, wk_row, dec_row):
        upd = _dot((kt * wk_row).astype(BF16), v1)
        s_ref[h] = s_ref[h] * dec_row + upd

    for t in range(nc_c):
        for (ci, s_ref, wk_ref, dec_ref) in ((t, sf_ref, cwkf_ref, cdecf_ref),
                                             (nc_c - 1 - t, sb_ref, cwkb_ref, cdecb_ref)):
            rows = slice(ci * LK, (ci + 1) * LK)
            for h in range(H):
                cols = slice(h * HEAD_DIM, (h + 1) * HEAD_DIM)
                kt = kc_ref[0, rows, cols].astype(F32).T
                v1 = jnp.concatenate([vc_ref[0, rows, cols], ones_blk], axis=1)
                state_update(s_ref, h, kt, v1, wk_ref[ci][h:h + 1, :], dec_ref[ci][h:h + 1, :])

    ii = lax.broadcasted_iota(jnp.int32, (LK, LK), 0)
    jj = lax.broadcasted_iota(jnp.int32, (LK, LK), 1)
    causal = jj <= ii
    anti = jj >= ii

    def one_direction(ci, h, mask, s_ref, c_ref, wk_ref, dec_ref, q0):
        rows = pl.ds(pl.multiple_of(ci * LK, LK), LK)
        cols = slice(h * HEAD_DIM, (h + 1) * HEAD_DIM)
        q = qx_ref[0, rows, cols]
        kt = kx_ref[0, rows, cols].astype(F32).T
        v1 = jnp.concatenate([vx_ref[0, rows, cols], ones_blk], axis=1)
        col = col_ref[ci]
        m_col = col[:, q0 * H + h:q0 * H + h + 1]
        wi_col = col[:, (q0 + 1) * H + h:(q0 + 1) * H + h + 1]
        en_col = col[:, (q0 + 2) * H + h:(q0 + 2) * H + h + 1]
        c_row = c_ref[ci][h:h + 1, :]
        s = _dot(q, kt.astype(BF16))
        p = jnp.where(mask, s * jnp.exp(c_row - m_col), 0.0).astype(BF16)
        o1 = _dot(p, v1)
        o2 = _dot(q, s_ref[h].astype(BF16))
        num = o1[:, :HEAD_DIM] + wi_col * o2[:, :HEAD_DIM]
        den = o1[:, HEAD_DIM:] + wi_col * o2[:, HEAD_DIM:]
        den = jnp.maximum(jnp.abs(den), en_col)
        o_ref[0, rows, cols] += num / den
        state_update(s_ref, h, kt, v1, wk_ref[ci][h:h + 1, :], dec_ref[ci][h:h + 1, :])

    def chunk_body(t, carry):
        for h in range(H):
            one_direction(t, h, causal, sf_ref, cf_ref, wkf_ref, decf_ref, 0)
            one_direction(nc_x - 1 - t, h, anti, sb_ref, cb_ref, wkb_ref, decb_ref, 3)
        return carry

    lax.fori_loop(0, nc_x, chunk_body, 0)

    def norm_body(t, carry):
        rows = pl.ds(pl.multiple_of(t * LK, LK), LK)
        for h in range(H):
            cols = slice(h * HEAD_DIM, (h + 1) * HEAD_DIM)
            hh = o_ref[0, rows, cols]
            o_ref[0, rows, cols] = hh * lax.rsqrt(jnp.mean(hh * hh, axis=-1, keepdims=True) + EPS)
        return carry

    lax.fori_loop(0, nc_x, norm_body, 0)


def _scan_call(qx, kx, vx, kc, vc, gtx, gtc):
    bsz, t, d = qx.shape
    tc = kc.shape[1]
    nc_x, nc_c = t // LK, tc // LK
    H = N_HEADS
    x_spec = pl.BlockSpec((1, t, d), lambda b: (b, 0, 0))
    c_spec = pl.BlockSpec((1, tc, d), lambda b: (b, 0, 0))
    row = lambda n, w: pltpu.VMEM((n, H, w), F32)
    return pl.pallas_call(
        functools.partial(_scan_kernel, nc_x=nc_x, nc_c=nc_c),
        grid=(bsz,),
        in_specs=[
            x_spec, x_spec, x_spec, c_spec, c_spec,
            pl.BlockSpec((1, N_GATES, t), lambda b: (b, 0, 0)),
            pl.BlockSpec((1, N_GATES, tc), lambda b: (b, 0, 0)),
        ],
        out_specs=pl.BlockSpec((1, t, d), lambda b: (b, 0, 0)),
        out_shape=jax.ShapeDtypeStruct((bsz, t, d), F32),
        scratch_shapes=[
            pltpu.VMEM((H, HEAD_DIM, 2 * HEAD_DIM), F32),
            pltpu.VMEM((H, HEAD_DIM, 2 * HEAD_DIM), F32),
            pltpu.VMEM((nc_x, LK, LK), F32),
            row(nc_x, LK), row(nc_x, LK),
            row(nc_x, LK), row(nc_x, LK),
            row(nc_x, 2 * LK), row(nc_x, 2 * LK),
            row(nc_c, LK), row(nc_c, LK),
            row(nc_c, 2 * LK), row(nc_c, 2 * LK),
        ],
        compiler_params=pltpu.CompilerParams(
            dimension_semantics=("arbitrary",), vmem_limit_bytes=_VMEM_LIMIT),
        name="mlstm_scan",
    )(qx, kx, vx, kc, vc, gtx, gtc)


def _mixer_kernel(x_ref, hn_ref, mod_ref, gn_ref, wr_ref, br_ref, wc_ref, bc_ref, gh_ref,
                  wpa_ref, wpb_ref, wo_ref, bo_ref, gf_ref, o_ref, *, tm):
    d = D_MODEL
    x = x_ref[0]
    hx = _modulated_norm(x, gn_ref[...], mod_ref).astype(BF16)

    def proj(i):
        return _dot(hx, wr_ref[:, i * d:(i + 1) * d]) + br_ref[:, i * d:(i + 1) * d]

    u = proj(3) * proj(4)
    pos = lax.broadcasted_iota(jnp.int32, (tm, 1), 0) % GRID_W
    u_prev = jnp.where(pos == 0, 0.0, pltpu.roll(u, 1, axis=0))
    u_next = jnp.where(pos == GRID_W - 1, 0.0, pltpu.roll(u, tm - 1, axis=0))
    conv = wc_ref[0:1, :] * u_prev + wc_ref[1:2, :] * u + wc_ref[2:3, :] * u_next + bc_ref[...]
    y_a = (proj(2) * conv * _silu(proj(5))).astype(BF16)
    merged = jax.nn.sigmoid(proj(6)) * _dot(y_a, wpa_ref[...])
    y_b = (jax.nn.sigmoid(proj(0)) * (hn_ref[0] * gh_ref[...]) * _silu(proj(1))).astype(BF16)
    merged = merged + jax.nn.sigmoid(proj(7)) * _dot(y_b, wpb_ref[...])
    out = _dot(merged.astype(BF16), wo_ref[...]) + bo_ref[...]
    xn = x + mod_ref[0, 2:3, :] * out
    ms = jnp.mean(xn * xn, axis=-1, keepdims=True)
    o_ref[0] = xn * lax.rsqrt(ms + EPS) * gf_ref[...]


def _mixer_call(x, hn, mod, gn, wr, br, wc, bc, gh, wpa, wpb, wo, bo, gf, *, tm):
    bsz, t, d = x.shape
    const = lambda b, i: (0, 0)
    single = dict(pipeline_mode=pl.Buffered(1))
    row_spec = pl.BlockSpec((1, d), const)
    tok_spec = pl.BlockSpec((1, tm, d), lambda b, i: (b, i, 0))
    sq_spec = pl.BlockSpec((d, d), const, **single)
    return pl.pallas_call(
        functools.partial(_mixer_kernel, tm=tm),
        grid=(bsz, t // tm),
        in_specs=[
            tok_spec, tok_spec,
            pl.BlockSpec((1, 3, d), lambda b, i: (b, 0, 0)),
            row_spec,
            pl.BlockSpec((d, 8 * d), const, **single),
            pl.BlockSpec((1, 8 * d), const),
            pl.BlockSpec((3, d), const),
            row_spec, row_spec,
            sq_spec, sq_spec, sq_spec,
            row_spec, row_spec,
        ],
        out_specs=tok_spec,
        out_shape=jax.ShapeDtypeStruct((bsz, t, d), F32),
        compiler_params=pltpu.CompilerParams(
            dimension_semantics=("arbitrary", "arbitrary"), vmem_limit_bytes=_VMEM_LIMIT),
        name="mixer",
    )(x, hn, mod, gn, wr, br, wc, bc, gh, wpa, wpb, wo, bo, gf)


def kernel(x, c, ctx, c_ctx, w_ada, b_ada, g_norm, w_in, b_in, w_conv, b_conv, g_head,
           w_pa, w_pb, w_out, b_out, g_final):
    depth = w_in.shape[0]
    assert depth == 1, "single-layer block: the context stream only supplies scan states"
    bsz, t, d = x.shape
    assert d == D_MODEL and t % LK == 0 and ctx.shape[1] % LK == 0

    cc = jnp.concatenate([c, c_ctx[None, :], jnp.zeros((MOD_ROWS - bsz - 1, d), F32)], axis=0)
    mod = _ada_call(cc, w_ada[0], b_ada[0][None, :])
    mod = mod.reshape(MOD_ROWS, 3, d)
    mod_x, mod_c = mod[:bsz], mod[bsz:bsz + 1]

    w = w_in[0].astype(BF16)
    b = b_in[0]
    wq, wk, wv = w[:, :d], w[:, d:2 * d], w[:, 2 * d:3 * d]
    wgt = w[:, 3 * d:STATE_COLS].T
    bq, bk, bv = b[None, :d], b[None, d:2 * d], b[None, 2 * d:3 * d]
    bg = b[3 * d:STATE_COLS, None]
    gn = g_norm[0][None, :]

    proj = functools.partial(_state_proj_call, gn=gn, wq=wq, wk=wk, wv=wv, wgt=wgt,
                             bq=bq, bk=bk, bv=bv, bg=bg)
    _, kc, vc, gtc = proj(ctx, mod_c, tm=ctx.shape[1], emit_q=False, per_batch_mod=False)
    qx, kx, vx, gtx = proj(x, mod_x, tm=512, emit_q=True, per_batch_mod=True)

    hn = _scan_call(qx, kx, vx, kc, vc, gtx, gtc)

    return _mixer_call(
        x, hn, mod_x, gn, w[:, STATE_COLS:], b[None, STATE_COLS:], w_conv[0], b_conv[0][None, :],
        g_head[0][None, :], w_pa[0].astype(BF16), w_pb[0].astype(BF16), w_out[0].astype(BF16),
        b_out[0][None, :], g_final[None, :], tm=512)
```
